```python
import math
import jax, jax.numpy as jnp
from jax import lax
import numpy as np

D_MODEL = 1024
BATCH = 4
SEQ = 8192
DEPTH = 1

GLA_HEADS = 4
GLA_DK = 64
GLA_DV = 128
GLA_GATE_RANK = 16
GLA_GATE_TAU = 16.0
GLA_CHUNK = 64
DIFF_HEADS = 4
DIFF_DK = 64
DIFF_DV = 2 * DIFF_DK
ROPE_DIM = DIFF_DK // 4
ROPE_THETA = 500000.0
Q_BLOCK = 128
MIX_WIDTH = GLA_HEADS * GLA_DV + DIFF_HEADS * DIFF_DV

IN_SPLITS = [
    GLA_HEADS * GLA_DK,
    GLA_HEADS * GLA_DK,
    GLA_HEADS * GLA_DV,
    GLA_GATE_RANK,
    GLA_HEADS * GLA_DV,
    2 * DIFF_HEADS * DIFF_DK,
    2 * DIFF_HEADS * DIFF_DK,
    DIFF_HEADS * DIFF_DV,
]
IN_WIDTH = int(sum(IN_SPLITS))
SPLIT_IDX = [int(v) for v in np.cumsum(IN_SPLITS)[:-1]]

PEER_HEADS = 8
PEER_NKEYS = 128
PEER_EXPERTS = PEER_NKEYS * PEER_NKEYS
PEER_QDIM = 256
PEER_TOPK = 16
PEER_TOKEN_BLOCK = 128

RMS_EPS = 1e-6

kernel_name = "hymba_gla_diffattn_peer_block"


def rms_norm(x, g):
    xf = x.astype(jnp.float32)
    y = xf * lax.rsqrt(jnp.mean(xf * xf, axis=-1, keepdims=True) + RMS_EPS)
    return (y * g.astype(jnp.float32)).astype(x.dtype)


def rope_tables(positions):
    inv_freq = ROPE_THETA ** (-jnp.arange(0, ROPE_DIM, 2, dtype=jnp.float32) / ROPE_DIM)
    ang = positions.astype(jnp.float32)[..., None] * inv_freq
    return jnp.cos(ang), jnp.sin(ang)


def partial_rope(t, cos, sin):
    half = ROPE_DIM // 2
    x1 = t[..., :half].astype(jnp.float32)
    x2 = t[..., half:ROPE_DIM].astype(jnp.float32)
    rot = jnp.concatenate([x1 * cos - x2 * sin, x2 * cos + x1 * sin], axis=-1).astype(t.dtype)
    return jnp.concatenate([rot, t[..., ROPE_DIM:]], axis=-1)


def gla_chunked(q, k, v, log_a):
    B, H, S, dk = q.shape
    dv = v.shape[-1]
    n = S // GLA_CHUNK
    f32 = jnp.float32
    q = q.astype(f32).reshape(B, H, n, GLA_CHUNK, dk)
    k = k.astype(f32).reshape(B, H, n, GLA_CHUNK, dk)
    v = v.astype(f32).reshape(B, H, n, GLA_CHUNK, dv)
    b = jnp.cumsum(log_a.astype(f32).reshape(B, H, n, GLA_CHUNK, dk), axis=-2)
    b_end = b[..., -1:, :]
    q_t = q * jnp.exp(b)
    k_t = k * jnp.exp(-b)
    causal = jnp.tril(jnp.ones((GLA_CHUNK, GLA_CHUNK), dtype=bool))
    att = jnp.where(causal, jnp.einsum('bhncd,bhnsd->bhncs', q_t, k_t), 0.0)
    o_intra = jnp.einsum('bhncs,bhnsv->bhncv', att, v)
    chunk_kv = jnp.einsum('bhncd,bhncv->bhndv', k * jnp.exp(b_end - b), v)
    decay_end = jnp.exp(b_end[..., 0, :])

    def step(state, inp):
        dec, kv = inp
        return dec[..., None] * state + kv, state

    init = jnp.zeros((B, H, dk, dv), f32)
    _, states = lax.scan(step, init, (jnp.moveaxis(decay_end, 2, 0), jnp.moveaxis(chunk_kv, 2, 0)))
    states = jnp.moveaxis(states, 0, 2)
    o_inter = jnp.einsum('bhncd,bhndv->bhncv', q_t, states)
    return (o_intra + o_inter).reshape(B, H, S, dv)


def gla_mixer(gq, gk, gv, ga, gr, w_gate, b_gate, g_out):
    B, S, _ = gq.shape
    to_heads = lambda t, d: t.reshape(B, S, GLA_HEADS, d).transpose(0, 2, 1, 3)
    q = to_heads(gq, GLA_DK) * (GLA_DK ** -0.5)
    k = to_heads(gk, GLA_DK)
    v = to_heads(gv, GLA_DV)
    log_a = jax.nn.log_sigmoid((ga @ w_gate + b_gate).astype(jnp.float32)) / GLA_GATE_TAU
    log_a = to_heads(log_a, GLA_DK)
    o = gla_chunked(q, k, v, log_a).astype(gv.dtype)
    o = rms_norm(o, g_out).transpose(0, 2, 1, 3).reshape(B, S, GLA_HEADS * GLA_DV)
    return o * jax.nn.silu(gr)


def diff_mixer(dq, dk, dv, cos, sin, g_q, g_k, lq1, lk1, lq2, lk2, g_sub, lambda_init):
    B, S, _ = dq.shape
    q = dq.reshape(B, S, DIFF_HEADS, 2, DIFF_DK).transpose(0, 2, 3, 1, 4)
    k = dk.reshape(B, S, DIFF_HEADS, 2, DIFF_DK).transpose(0, 2, 3, 1, 4)
    v = dv.reshape(B, S, DIFF_HEADS, DIFF_DV).transpose(0, 2, 1, 3)
    c = cos[:, None, None]
    s_ = sin[:, None, None]
    q = partial_rope(rms_norm(q, g_q), c, s_)
    k = partial_rope(rms_norm(k, g_k), c, s_)
    lam = (jnp.exp(jnp.sum(lq1.astype(jnp.float32) * lk1.astype(jnp.float32)))
           - jnp.exp(jnp.sum(lq2.astype(jnp.float32) * lk2.astype(jnp.float32))) + lambda_init)
    scale = DIFF_DK ** -0.5
    nb = S // Q_BLOCK
    qb = jnp.moveaxis(q.reshape(B, DIFF_HEADS, 2, nb, Q_BLOCK, DIFF_DK), 3, 0)
    kidx = jnp.arange(S)

    def block(args):
        qblk, i = args
        sc = jnp.einsum('bhmqd,bhmkd->bhmqk', qblk, k).astype(jnp.float32) * scale
        qidx = i * Q_BLOCK + jnp.arange(Q_BLOCK)
        sc = jnp.where(kidx[None, :] <= qidx[:, None], sc, -jnp.inf)
        p = jax.nn.softmax(sc, axis=-1)
        a = p[:, :, 0] - lam * p[:, :, 1]
        return jnp.einsum('bhqk,bhkv->bhqv', a.astype(v.dtype), v)

    o = lax.map(block, (qb, jnp.arange(nb)))
    o = jnp.moveaxis(o, 0, 2).reshape(B, DIFF_HEADS, S, DIFF_DV)
    o = rms_norm(o, g_sub) * (1.0 - lambda_init)
    return o.transpose(0, 2, 1, 3).reshape(B, S, DIFF_HEADS * DIFF_DV)


def peer_mixer(x2, w_q, sub_keys, u, v):
    B, S, D = x2.shape
    T = B * S
    xt = x2.reshape(T, D)
    q = (xt @ w_q).reshape(T, PEER_HEADS, 2, PEER_QDIM // 2)
    s = jnp.einsum('thpd,hpkd->thpk', q, sub_keys).astype(jnp.float32)
    s_top, i_top = lax.top_k(s, PEER_TOPK)
    cand_s = (s_top[:, :, 0, :, None] + s_top[:, :, 1, None, :]).reshape(T, PEER_HEADS, PEER_TOPK * PEER_TOPK)
    cand_i = (i_top[:, :, 0, :, None] * PEER_NKEYS + i_top[:, :, 1, None, :]).reshape(T, PEER_HEADS, PEER_TOPK * PEER_TOPK)
    best_s, best_pos = lax.top_k(cand_s, PEER_TOPK)
    idx = jnp.take_along_axis(cand_i, best_pos, axis=-1)
    g = jax.nn.softmax(best_s, axis=-1).astype(x2.dtype)
    nblk = T // PEER_TOKEN_BLOCK
    E = PEER_HEADS * PEER_TOPK
    xb = xt.reshape(nblk, PEER_TOKEN_BLOCK, D)
    ib = idx.reshape(nblk, PEER_TOKEN_BLOCK, E)
    gb = g.reshape(nblk, PEER_TOKEN_BLOCK, E)

    def block(args):
        xc, ic, gc = args
        h = jnp.einsum('td,ted->te', xc, jnp.take(u, ic, axis=0))
        a = jax.nn.gelu(h, approximate=False) * gc
        return jnp.einsum('te,ted->td', a, jnp.take(v, ic, axis=0))

    return lax.map(block, (xb, ib, gb)).reshape(B, S, D)


def setup_inputs(seed: int = 0) -> dict:
    key = jax.random.key(seed)
    ks = jax.random.split(key, 24)
    f32 = jnp.float32
    L, D = DEPTH, D_MODEL
    nrm = lambda k, shape, sc: jax.random.normal(k, shape, f32) * sc
    x = jax.random.normal(ks[0], (BATCH, SEQ, D), f32)
    offsets = jax.random.randint(ks[1], (BATCH, 1), 0, 4096, dtype=jnp.int32)
    positions = (jnp.arange(SEQ, dtype=jnp.int32)[None, :] + offsets).astype(jnp.int32)
    return {
        "x": x,
        "positions": positions,
        "g_mix": 1.0 + nrm(ks[2], (L, D), 0.1),
        "w_in": nrm(ks[3], (L, D, IN_WIDTH), D ** -0.5),
        "w_gla_gate": nrm(ks[4], (L, GLA_GATE_RANK, GLA_HEADS * GLA_DK), GLA_GATE_RANK ** -0.5),
        "b_gla_gate": nrm(ks[5], (L, GLA_HEADS * GLA_DK), 0.1),
        "g_gla_out": 1.0 + nrm(ks[6], (L, GLA_DV), 0.1),
        "g_q": 1.0 + nrm(ks[7], (L, DIFF_DK), 0.1),
        "g_k": 1.0 + nrm(ks[8], (L, DIFF_DK), 0.1),
        "lambda_q1": nrm(ks[9], (L, DIFF_DK), 0.1),
        "lambda_k1": nrm(ks[10], (L, DIFF_DK), 0.1),
        "lambda_q2": nrm(ks[11], (L, DIFF_DK), 0.1),
        "lambda_k2": nrm(ks[12], (L, DIFF_DK), 0.1),
        "g_diff_sub": 1.0 + nrm(ks[13], (L, DIFF_DV), 0.1),
        "w_out": nrm(ks[14], (L, MIX_WIDTH, D), MIX_WIDTH ** -0.5),
        "g_ffn": 1.0 + nrm(ks[15], (L, D), 0.1),
        "w_peer_q": nrm(ks[16], (L, D, PEER_HEADS * PEER_QDIM), D ** -0.5),
        "peer_sub_keys": nrm(ks[17], (L, PEER_HEADS, 2, PEER_NKEYS, PEER_QDIM // 2), (PEER_QDIM // 2) ** -0.5),
        "peer_u": nrm(ks[18], (L, PEER_EXPERTS, D), D ** -0.5),
        "peer_v": nrm(ks[19], (L, PEER_EXPERTS, D), 0.5),
    }


def reference(x, positions, g_mix, w_in, w_gla_gate, b_gla_gate, g_gla_out, g_q, g_k,
              lambda_q1, lambda_k1, lambda_q2, lambda_k2, g_diff_sub, w_out, g_ffn,
              w_peer_q, peer_sub_keys, peer_u, peer_v):
    cos, sin = rope_tables(positions)
    h = x
    for l in range(DEPTH):
        lambda_init = 0.8 - 0.6 * math.exp(-0.3 * l)
        xn = rms_norm(h, g_mix[l])
        proj = xn @ w_in[l]
        gq, gk, gv, ga, gr, dq, dk, dv = jnp.split(proj, SPLIT_IDX, axis=-1)
        o_gla = gla_mixer(gq, gk, gv, ga, gr, w_gla_gate[l], b_gla_gate[l], g_gla_out[l])
        o_diff = diff_mixer(dq, dk, dv, cos, sin, g_q[l], g_k[l], lambda_q1[l], lambda_k1[l],
                            lambda_q2[l], lambda_k2[l], g_diff_sub[l], lambda_init)
        h = h + jnp.concatenate([o_gla, o_diff], axis=-1) @ w_out[l]
        h = h + peer_mixer(rms_norm(h, g_ffn[l]), w_peer_q[l], peer_sub_keys[l], peer_u[l], peer_v[l])
    return h
```

```python
import functools
import math

import numpy as np
import jax
import jax.numpy as jnp
from jax import lax
from jax.experimental import pallas as pl
from jax.experimental.pallas import tpu as pltpu

F32 = jnp.float32
BF16 = jnp.bfloat16

D_MODEL = 1024
GLA_HEADS = 4
GLA_DK = 64
GLA_DV = 128
GLA_GATE_RANK = 16
GLA_GATE_TAU = 16.0
GLA_CHUNK = 64
DIFF_HEADS = 4
DIFF_DK = 64
DIFF_DV = 128
ROPE_DIM = 16
ROPE_THETA = 500000.0
PEER_HEADS = 8
PEER_NKEYS = 128
PEER_EXPERTS = PEER_NKEYS * PEER_NKEYS
PEER_HALF = 128
PEER_TOPK = 16
RMS_EPS = 1e-6

LANES = 128
HEAD_W = GLA_HEADS * LANES
VMEM_LIMIT = 56 * 1024 * 1024

C_GQ, C_GK, C_GV, C_GR, C_DQ, C_DK, C_DV, C_GA = (HEAD_W * i for i in range(8))
C_END = C_GA + LANES

NT_DIMS = (((1,), (1,)), ((), ()))
TN_DIMS = (((0,), (0,)), ((), ()))


def _split2(a):
    hi = a.astype(BF16)
    lo = (a - hi.astype(F32)).astype(BF16)
    return hi, lo


def _split3(a):
    hi = a.astype(BF16)
    r1 = a - hi.astype(F32)
    mid = r1.astype(BF16)
    lo = (r1 - mid.astype(F32)).astype(BF16)
    return hi, mid, lo


def _exact_left_dot(p01, a):
    return sum(jnp.dot(p01, piece, preferred_element_type=F32) for piece in _split3(a))


def _inproj_body(x_ref, pos_ref, gmix_ref, w_ref, wg_ref, bg_ref, gq_ref, gk_ref, freq_ref, sign_ref, gsum_ref,
                 oq_ref, ok_ref, ov_ref, or_ref, ola_ref, odq_ref, odk_ref, odv_ref):
    x = x_ref[...]
    xn = x * lax.rsqrt(jnp.mean(x * x, axis=-1, keepdims=True) + RMS_EPS) * gmix_ref[...]
    proj = jnp.dot(xn.astype(BF16), w_ref[...], preferred_element_type=F32)
    oq_ref[...] = proj[:, C_GQ:C_GK]
    ok_ref[...] = proj[:, C_GK:C_GV]
    ov_ref[...] = proj[:, C_GV:C_GR].astype(BF16)
    or_ref[...] = proj[:, C_GR:C_DQ]
    odv_ref[...] = proj[:, C_DV:C_GA].astype(BF16)

    ga = proj[:, C_GA:C_END].astype(BF16)
    z = jnp.dot(ga, wg_ref[...], preferred_element_type=F32) + bg_ref[...]
    ola_ref[...] = (jnp.minimum(z, 0.0) - jnp.log1p(jnp.exp(-jnp.abs(z)))) * (1.0 / GLA_GATE_TAU)

    ang = pos_ref[...].astype(F32) * freq_ref[...]
    cos = jnp.cos(ang)
    sin = jnp.sin(ang) * sign_ref[...]
    lane = lax.broadcasted_iota(jnp.int32, (1, LANES), 1)
    first_half = (lane % DIFF_DK) < (ROPE_DIM // 2)

    def norm_rope(t, g):
        hi, lo = _split2(t * t)
        gs = gsum_ref[...]
        ss = jnp.dot(hi, gs, preferred_element_type=F32) + jnp.dot(lo, gs, preferred_element_type=F32)
        tn = t * lax.rsqrt(ss * (1.0 / DIFF_DK) + RMS_EPS) * g
        outs = []
        for c in range(HEAD_W // LANES):
            tc = tn[:, c * LANES:(c + 1) * LANES]
            partner = jnp.where(first_half, pltpu.roll(tc, LANES - ROPE_DIM // 2, 1),
                                pltpu.roll(tc, ROPE_DIM // 2, 1))
            outs.append(tc * cos + partner * sin)
        return jnp.concatenate(outs, axis=1)

    odq_ref[...] = (norm_rope(proj[:, C_DQ:C_DK], gq_ref[...]) * (DIFF_DK ** -0.5)).astype(BF16)
    odk_ref[...] = norm_rope(proj[:, C_DK:C_DV], gk_ref[...]).astype(BF16)


def _inproj(x2d, pos2d, gmix, w_all, wg, bg, gq_t, gk_t, freq, sign, gsum, *, tm, interpret=False):
    T = x2d.shape[0]
    full = lambda a: pl.BlockSpec(a.shape, lambda i: (0,) * a.ndim)
    row = lambda w: pl.BlockSpec((tm, w), lambda i: (i, 0))
    outs = [
        jax.ShapeDtypeStruct((T, HEAD_W), F32),
        jax.ShapeDtypeStruct((T, HEAD_W), F32),
        jax.ShapeDtypeStruct((T, HEAD_W), BF16),
        jax.ShapeDtypeStruct((T, HEAD_W), F32),
        jax.ShapeDtypeStruct((T, HEAD_W), F32),
        jax.ShapeDtypeStruct((T, HEAD_W), BF16),
        jax.ShapeDtypeStruct((T, HEAD_W), BF16),
        jax.ShapeDtypeStruct((T, HEAD_W), BF16),
    ]
    return pl.pallas_call(
        _inproj_body,
        out_shape=outs,
        grid=(T // tm,),
        in_specs=[row(D_MODEL), pl.BlockSpec((tm, 1), lambda i: (i, 0)), full(gmix), full(w_all), full(wg), full(bg),
                  full(gq_t), full(gk_t), full(freq), full(sign), full(gsum)],
        out_specs=[row(HEAD_W)] * 8,
        compiler_params=pltpu.CompilerParams(dimension_semantics=("arbitrary",), vmem_limit_bytes=VMEM_LIMIT),
        name="inproj",
        interpret=interpret,
    )(x2d, pos2d, gmix, w_all, wg, bg, gq_t, gk_t, freq, sign, gsum)


def _gla_body(q_ref, k_ref, v_ref, r_ref, la_ref, gout_ref, tri_ref, o_ref, st_ref, *, n_chunks):
    @pl.when(pl.program_id(1) == 0)
    def _():
        st_ref[...] = jnp.zeros_like(st_ref)

    row = lax.broadcasted_iota(jnp.int32, (GLA_CHUNK, GLA_CHUNK), 0)
    col = lax.broadcasted_iota(jnp.int32, (GLA_CHUNK, GLA_CHUNK), 1)
    causal = row >= col
    tri = tri_ref[...]
    gout = gout_ref[...]

    def chunk(c, carry):
        rows = pl.ds(pl.multiple_of(c * GLA_CHUNK, GLA_CHUNK), GLA_CHUNK)
        b = _exact_left_dot(tri, la_ref[rows, :])
        b_end = b[GLA_CHUNK - 1:GLA_CHUNK, :]
        k = k_ref[rows, :]
        qt = (q_ref[rows, :] * (GLA_DK ** -0.5) * jnp.exp(b)).astype(BF16)
        kt = (k * jnp.exp(-b)).astype(BF16)
        kend = (k * jnp.exp(b_end - b)).astype(BF16)
        dec = jnp.exp(b_end)
        v = v_ref[rows, :]
        r = r_ref[rows, :]
        for h in range(GLA_HEADS):
            sl = slice(h * LANES, (h + 1) * LANES)
            att = lax.dot_general(qt[:, sl], kt[:, sl], NT_DIMS, preferred_element_type=F32)
            att = jnp.where(causal, att, 0.0)
            st = st_ref[h]
            o = (jnp.dot(att.astype(BF16), v[:, sl], preferred_element_type=F32)
                 + lax.dot_general(qt[:, sl], st.astype(BF16), NT_DIMS, preferred_element_type=F32))
            kv_t = lax.dot_general(v[:, sl], kend[:, sl], TN_DIMS, preferred_element_type=F32)
            st_ref[h] = st * dec[:, sl] + kv_t
            on = o * lax.rsqrt(jnp.mean(o * o, axis=-1, keepdims=True) + RMS_EPS) * gout
            rh = r[:, sl]
            o_ref[rows, sl] = (on * (rh * jax.nn.sigmoid(rh))).astype(BF16)
        return carry

    lax.fori_loop(0, n_chunks, chunk, 0)


def _gla(gq, gk, gv, gr, la, gout, tri, *, batch, seq, tile, interpret=False):
    T = gq.shape[0]
    nt = seq // tile
    row = pl.BlockSpec((tile, HEAD_W), lambda b, i: (b * nt + i, 0))
    full = lambda a: pl.BlockSpec(a.shape, lambda b, i: (0,) * a.ndim)
    return pl.pallas_call(
        functools.partial(_gla_body, n_chunks=tile // GLA_CHUNK),
        out_shape=jax.ShapeDtypeStruct((T, HEAD_W), BF16),
        grid=(batch, nt),
        in_specs=[row, row, row, row, row, full(gout), full(tri)],
        out_specs=row,
        scratch_shapes=[pltpu.VMEM((GLA_HEADS, GLA_DV, LANES), F32)],
        compiler_params=pltpu.CompilerParams(dimension_semantics=("arbitrary", "arbitrary"),
                                             vmem_limit_bytes=VMEM_LIMIT),
        name="gla",
        interpret=interpret,
    )(gq, gk, gv, gr, la, gout, tri)


def _diff_body(q_ref, k_ref, v_ref, lq1_ref, lk1_ref, lq2_ref, lk2_ref, gsub_ref, o_ref,
               m_ref, l_ref, acc_ref, *, blk, lambda_init):
    qi = pl.program_id(2)
    q = q_ref[...]
    lane = lax.broadcasted_iota(jnp.int32, (1, LANES), 1)
    low = lane < DIFF_DK
    zero = jnp.zeros((), BF16)

    m_ref[...] = jnp.full_like(m_ref, -jnp.inf)
    l_ref[...] = jnp.zeros_like(l_ref)
    acc_ref[...] = jnp.zeros_like(acc_ref)

    def step(j, masked):
        rows = pl.ds(pl.multiple_of(j * blk, blk), blk)
        kb = k_ref[rows, :]
        kz = jnp.concatenate([jnp.where(low, kb, zero), jnp.where(low, zero, kb)], axis=0)
        s = lax.dot_general(q, kz, NT_DIMS, preferred_element_type=F32)
        if masked:
            r = lax.broadcasted_iota(jnp.int32, (blk, blk), 0)
            c = lax.broadcasted_iota(jnp.int32, (blk, blk), 1)
            keep = r >= c
        ps, alphas = [], []
        for mp in range(2):
            sm = s[:, mp * blk:(mp + 1) * blk]
            if masked:
                sm = jnp.where(keep, sm, -jnp.inf)
            m_prev = m_ref[mp]
            m_new = jnp.maximum(m_prev, jnp.max(sm, axis=1, keepdims=True))
            alpha = jnp.exp(m_prev - m_new)
            p = jnp.exp(sm - m_new)
            l_ref[mp] = alpha * l_ref[mp] + jnp.sum(p, axis=1, keepdims=True)
            m_ref[mp] = m_new
            ps.append(p.astype(BF16))
            alphas.append(alpha)
        pv = jnp.dot(jnp.concatenate(ps, axis=0), v_ref[rows, :], preferred_element_type=F32)
        acc_ref[0] = alphas[0] * acc_ref[0] + pv[:blk]
        acc_ref[1] = alphas[1] * acc_ref[1] + pv[blk:]

    def body(j, carry):
        step(j, False)
        return carry

    lax.fori_loop(0, qi, body, 0)
    step(qi, True)

    lam = (jnp.exp(jnp.sum(lq1_ref[...] * lk1_ref[...], axis=-1, keepdims=True))
           - jnp.exp(jnp.sum(lq2_ref[...] * lk2_ref[...], axis=-1, keepdims=True)) + lambda_init)
    o = acc_ref[0] / l_ref[0] - lam * (acc_ref[1] / l_ref[1])
    on = o * lax.rsqrt(jnp.mean(o * o, axis=-1, keepdims=True) + RMS_EPS) * gsub_ref[...]
    o_ref[...] = (on * (1.0 - lambda_init)).astype(BF16)


def _diff_attn(dq, dk, dv, lq1, lk1, lq2, lk2, gsub, *, batch, seq, blk, lambda_init, interpret=False):
    T = dq.shape[0]
    nq = seq // blk
    full = lambda a: pl.BlockSpec(a.shape, lambda b, h, i: (0,) * a.ndim)
    qspec = pl.BlockSpec((blk, LANES), lambda b, h, i: (b * nq + i, h))
    kvspec = pl.BlockSpec((seq, LANES), lambda b, h, i: (b, h))
    return pl.pallas_call(
        functools.partial(_diff_body, blk=blk, lambda_init=lambda_init),
        out_shape=jax.ShapeDtypeStruct((T, HEAD_W), BF16),
        grid=(batch, DIFF_HEADS, nq),
        in_specs=[qspec, kvspec, kvspec, full(lq1), full(lk1), full(lq2), full(lk2), full(gsub)],
        out_specs=qspec,
        scratch_shapes=[pltpu.VMEM((2, blk, 1), F32), pltpu.VMEM((2, blk, 1), F32),
                        pltpu.VMEM((2, blk, DIFF_DV), F32)],
        compiler_params=pltpu.CompilerParams(dimension_semantics=("arbitrary", "arbitrary", "arbitrary"),
                                             vmem_limit_bytes=VMEM_LIMIT),
        name="diff_attn",
        interpret=interpret,
    )(dq, dk, dv, lq1, lk1, lq2, lk2, gsub)


def _top16(s, rowid, rid16):
    n_rows = s.shape[0]
    cur = s
    rank = jnp.full(s.shape, float(PEER_TOPK), F32)
    vals = jnp.zeros((PEER_TOPK, s.shape[1]), F32)
    for r in range(PEER_TOPK):
        m = jnp.max(cur, axis=0, keepdims=True)
        first = jnp.min(jnp.where(cur == m, rowid, float(n_rows)), axis=0, keepdims=True)
        sel = rowid == first
        rank = jnp.where(sel, float(r), rank)
        vals = jnp.where(rid16 == float(r), m, vals)
        cur = jnp.where(sel, -jnp.inf, cur)
    return vals, rank


def _route_body(og_ref, od_ref, x_ref, wo_ref, gffn_ref, wq_ref, keys_ref, pa_ref, pb_ref, cnt_ref,
                h1_ref, x2_ref, r0_ref, n0_ref, r1_ref, k1_ref, q_scr, *, tm):
    a = jnp.concatenate([og_ref[...], od_ref[...]], axis=1)
    h1 = x_ref[...] + jnp.dot(a, wo_ref[...], preferred_element_type=F32)
    h1_ref[...] = h1
    x2 = (h1 * lax.rsqrt(jnp.mean(h1 * h1, axis=-1, keepdims=True) + RMS_EPS) * gffn_ref[...]).astype(BF16)
    x2_ref[...] = x2
    q_scr[...] = jnp.dot(x2, wq_ref[...], preferred_element_type=F32).astype(BF16)

    rowid = lax.broadcasted_iota(jnp.int32, (PEER_NKEYS, tm), 0).astype(F32)
    rid16 = lax.broadcasted_iota(jnp.int32, (PEER_TOPK, tm), 0).astype(F32)
    n_real = float(pa_ref.shape[0] - PEER_TOPK)

    def head(h, carry):
        tops = []
        for p in range(2):
            cols = pl.ds(pl.multiple_of((2 * h + p) * PEER_HALF, PEER_HALF), PEER_HALF)
            s_t = lax.dot_general(keys_ref[2 * h + p], q_scr[:, cols], NT_DIMS, preferred_element_type=F32)
            vals, rank = _top16(s_t, rowid, rid16)
            tops.append((s_t, vals, rank))
        (s0, a, rank0), (s1, b, rank1) = tops
        cand = _exact_left_dot(pa_ref[...], a) + _exact_left_dot(pb_ref[...], b)
        cand = jnp.where(rowid < n_real, cand, -jnp.inf)
        _, crank = _top16(cand, rowid, rid16)
        sel = crank < float(PEER_TOPK)
        n_by_x = jnp.dot(cnt_ref[...], sel.astype(BF16), preferred_element_type=F32)
        z = jnp.sum(jnp.where(sel, jnp.exp(cand - cand[0:1, :]), 0.0), axis=0, keepdims=True)
        n0 = jnp.zeros((PEER_NKEYS, tm), F32)
        for x in range(PEER_TOPK):
            n0 = jnp.where(rank0 == float(x), n_by_x[x:x + 1, :], n0)
        r0_ref[h] = jnp.where(rank0 < float(PEER_TOPK), jnp.exp(s0 - a[0:1, :]), 0.0) / z
        n0_ref[h] = n0
        r1_ref[h] = jnp.where(rank1 < float(PEER_TOPK), jnp.exp(s1 - b[0:1, :]), 0.0)
        k1_ref[h] = rank1
        return carry

    lax.fori_loop(0, PEER_HEADS, head, 0)


def _route(o_gla, o_diff, x2d, w_out, gffn, wq, keys, pa, pb, cnt, *, tm, interpret=False):
    T = x2d.shape[0]
    full = lambda a: pl.BlockSpec(a.shape, lambda i: (0,) * a.ndim)
    row = lambda w: pl.BlockSpec((tm, w), lambda i: (i, 0))
    gate = pl.BlockSpec((PEER_HEADS, PEER_NKEYS, tm), lambda i: (0, 0, i))
    gate_shape = jax.ShapeDtypeStruct((PEER_HEADS, PEER_NKEYS, T), F32)
    return pl.pallas_call(
        functools.partial(_route_body, tm=tm),
        out_shape=[jax.ShapeDtypeStruct((T, D_MODEL), F32), jax.ShapeDtypeStruct((T, D_MODEL), BF16),
                   gate_shape, gate_shape, gate_shape, gate_shape],
        grid=(T // tm,),
        in_specs=[row(HEAD_W), row(HEAD_W), row(D_MODEL), full(w_out), full(gffn), full(wq), full(keys),
                  full(pa), full(pb), full(cnt)],
        out_specs=[row(D_MODEL), row(D_MODEL), gate, gate, gate, gate],
        scratch_shapes=[pltpu.VMEM((tm, 2 * PEER_HEADS * PEER_HALF), BF16)],
        compiler_params=pltpu.CompilerParams(dimension_semantics=("arbitrary",), vmem_limit_bytes=VMEM_LIMIT),
        name="peer_route",
        interpret=interpret,
    )(o_gla, o_diff, x2d, w_out, gffn, wq, keys, pa, pb, cnt)


def _experts_body(x2_ref, u_ref, vt_ref, r0_ref, n0_ref, r1_ref, k1_ref, h1_ref, o_ref, acc_ref, *, te, tm):
    e = pl.program_id(1)

    @pl.when(e == 0)
    def _():
        acc_ref[...] = jnp.zeros_like(acc_ref)

    h_t = lax.dot_general(u_ref[...], x2_ref[...], NT_DIMS, preferred_element_type=F32)
    blocks = []
    for ib in range(te // PEER_NKEYS):
        i = e * (te // PEER_NKEYS) + ib
        g = jnp.zeros((PEER_NKEYS, tm), F32)
        for h in range(PEER_HEADS):
            n0 = n0_ref[h, pl.ds(i, 1), :]
            r0 = r0_ref[h, pl.ds(i, 1), :]
            g = g + jnp.where(k1_ref[h] < n0, r1_ref[h], 0.0) * r0
        hb = h_t[ib * PEER_NKEYS:(ib + 1) * PEER_NKEYS, :]
        act = 0.5 * hb * (1.0 + lax.erf(hb * (2.0 ** -0.5)))
        blocks.append((act * g).astype(BF16))
    a_t = jnp.concatenate(blocks, axis=0) if len(blocks) > 1 else blocks[0]
    acc_ref[...] += jnp.dot(vt_ref[...], a_t, preferred_element_type=F32)

    @pl.when(e == pl.num_programs(1) - 1)
    def _():
        o_ref[...] = h1_ref[...] + acc_ref[...].T


def _experts(x2, u, vt, r0, n0, r1, k1, h1, *, tm, te, interpret=False):
    T = x2.shape[0]
    tok = pl.BlockSpec((tm, D_MODEL), lambda t, e: (t, 0))
    gate = pl.BlockSpec((PEER_HEADS, PEER_NKEYS, tm), lambda t, e: (0, 0, t))
    return pl.pallas_call(
        functools.partial(_experts_body, te=te, tm=tm),
        out_shape=jax.ShapeDtypeStruct((T, D_MODEL), F32),
        grid=(T // tm, PEER_EXPERTS // te),
        in_specs=[tok, pl.BlockSpec((te, D_MODEL), lambda t, e: (e, 0)),
                  pl.BlockSpec((D_MODEL, te), lambda t, e: (0, e)), gate, gate, gate, gate, tok],
        out_specs=tok,
        scratch_shapes=[pltpu.VMEM((D_MODEL, tm), F32)],
        compiler_params=pltpu.CompilerParams(dimension_semantics=("arbitrary", "arbitrary"),
                                             vmem_limit_bytes=VMEM_LIMIT),
        name="peer_experts",
        interpret=interpret,
    )(x2, u, vt, r0, n0, r1, k1, h1)


def _pad_heads(w, heads, d):
    lead = w.shape[:-1]
    w = w.reshape(lead + (heads, d))
    w = jnp.pad(w, [(0, 0)] * len(lead) + [(0, 0), (0, LANES - d)])
    return w.reshape(lead + (heads * LANES,))


def _candidate_tables():
    pairs = [(x, y) for x in range(4) for y in range(16)] + [(x, y) for x in range(4, 16) for y in range(4)]
    assert pairs == sorted(pairs) and len(pairs) == PEER_NKEYS - PEER_TOPK
    pa = np.zeros((PEER_NKEYS, PEER_TOPK), np.float32)
    pb = np.zeros((PEER_NKEYS, PEER_TOPK), np.float32)
    cnt = np.zeros((PEER_TOPK, PEER_NKEYS), np.float32)
    for r, (x, y) in enumerate(pairs):
        pa[r, x] = 1.0
        pb[r, y] = 1.0
        cnt[x, r] = 1.0
    return jnp.asarray(pa, BF16), jnp.asarray(pb, BF16), jnp.asarray(cnt, BF16)


def _forward(x, positions, g_mix, w_in, w_gla_gate, b_gla_gate, g_gla_out, g_q, g_k, lambda_q1, lambda_k1,
             lambda_q2, lambda_k2, g_diff_sub, w_out, g_ffn, w_peer_q, peer_sub_keys, peer_u, peer_v,
             *, tm_in, gla_tile, attn_blk, tm_route, tm_exp, te_exp, interpret=False):
    B, S, D = x.shape
    T = B * S
    assert D == D_MODEL and w_in.shape[0] == 1
    l = 0
    lambda_init = 0.8 - 0.6 * math.exp(-0.3 * l)

    wi = w_in[l]
    o = np.cumsum([0, 256, 256, 512, 16, 512, 512, 512, 512])
    gq_w, gk_w, gv_w, ga_w, gr_w, dq_w, dk_w, dv_w = (wi[:, o[i]:o[i + 1]] for i in range(8))
    w_all = jnp.concatenate([
        _pad_heads(gq_w, GLA_HEADS, GLA_DK), _pad_heads(gk_w, GLA_HEADS, GLA_DK), gv_w, gr_w, dq_w, dk_w, dv_w,
        jnp.pad(ga_w, ((0, 0), (0, LANES - GLA_GATE_RANK)))], axis=1).astype(BF16)
    wg = jnp.pad(_pad_heads(w_gla_gate[l], GLA_HEADS, GLA_DK), ((0, LANES - GLA_GATE_RANK), (0, 0))).astype(BF16)
    bg = _pad_heads(b_gla_gate[l], GLA_HEADS, GLA_DK)[None, :]
    gq_t = jnp.tile(g_q[l], 2 * DIFF_HEADS)[None, :]
    gk_t = jnp.tile(g_k[l], 2 * DIFF_HEADS)[None, :]
    inv_freq = ROPE_THETA ** (-jnp.arange(0, ROPE_DIM, 2, dtype=F32) / ROPE_DIM)
    lane = np.arange(LANES) % DIFF_DK
    freq = jnp.where(lane < ROPE_DIM, jnp.tile(inv_freq, LANES // (ROPE_DIM // 2)), 0.0)[None, :]
    sign = jnp.asarray(np.where(lane < ROPE_DIM // 2, -1.0, np.where(lane < ROPE_DIM, 1.0, 0.0)), F32)[None, :]
    gsum = jnp.asarray(np.kron(np.eye(HEAD_W // DIFF_DK), np.ones((DIFF_DK, DIFF_DK))), BF16)
    tri = jnp.asarray(np.tril(np.ones((GLA_CHUNK, GLA_CHUNK))), BF16)
    keys = peer_sub_keys[l].reshape(2 * PEER_HEADS, PEER_NKEYS, PEER_HALF).astype(BF16)
    pa, pb, cnt = _candidate_tables()
    u_bf = peer_u[l].astype(BF16)
    vt_bf = peer_v[l].T.astype(BF16)

    x2d = x.reshape(T, D)
    pos2d = positions.reshape(T, 1)

    gq, gk, gv, gr, la, dq, dk, dv = _inproj(x2d, pos2d, g_mix[l][None, :], w_all, wg, bg, gq_t, gk_t, freq, sign,
                                             gsum, tm=tm_in, interpret=interpret)
    o_gla = _gla(gq, gk, gv, gr, la, g_gla_out[l][None, :], tri, batch=B, seq=S, tile=gla_tile, interpret=interpret)
    o_diff = _diff_attn(dq, dk, dv, lambda_q1[l][None, :], lambda_k1[l][None, :], lambda_q2[l][None, :],
                        lambda_k2[l][None, :], g_diff_sub[l][None, :], batch=B, seq=S, blk=attn_blk,
                        lambda_init=lambda_init, interpret=interpret)
    h1, x2, r0, n0, r1, k1 = _route(o_gla, o_diff, x2d, w_out[l].astype(BF16), g_ffn[l][None, :],
                                    w_peer_q[l].astype(BF16), keys, pa, pb, cnt, tm=tm_route, interpret=interpret)
    out = _experts(x2, u_bf, vt_bf, r0, n0, r1, k1, h1, tm=tm_exp, te=te_exp, interpret=interpret)
    return out.reshape(B, S, D)


def kernel(x, positions, g_mix, w_in, w_gla_gate, b_gla_gate, g_gla_out, g_q, g_k, lambda_q1, lambda_k1, lambda_q2, lambda_k2, g_diff_sub, w_out, g_ffn, w_peer_q, peer_sub_keys, peer_u, peer_v):
    return _forward(x, positions, g_mix, w_in, w_gla_gate, b_gla_gate, g_gla_out, g_q, g_k, lambda_q1, lambda_k1,
                    lambda_q2, lambda_k2, g_diff_sub, w_out, g_ffn, w_peer_q, peer_sub_keys, peer_u, peer_v,
                    tm_in=512, gla_tile=512, attn_blk=512, tm_route=256, tm_exp=512, te_exp=256)
```
